```python
import math
import jax, jax.numpy as jnp
from jax import lax
import numpy as np

D_MODEL = 2048
BATCH = 4
SEQ = 4096
DEPTH = 2

N_A_LAYERS = DEPTH // 2
N_B_LAYERS = DEPTH - N_A_LAYERS
CONV_WIDTH = 3
D_FF = ((8 * D_MODEL // 3 + 255) // 256) * 256
DIFF_HEADS = D_MODEL // 256
DIFF_HEAD_DIM = D_MODEL // DIFF_HEADS // 2
V_HEAD_DIM = 2 * DIFF_HEAD_DIM
QK_WIDTH = DIFF_HEADS * 2 * DIFF_HEAD_DIM
V_WIDTH = DIFF_HEADS * V_HEAD_DIM
ROT_DIM = DIFF_HEAD_DIM // 4
ROPE_THETA = 500000.0
Q_BLOCK = 128
DEEPNORM_ALPHA = (2.0 * DEPTH) ** 0.25
DEEPNORM_BETA = (8.0 * DEPTH) ** -0.25
LN_EPS = 1e-5
SUBLN_EPS = 1e-5

kernel_name = 'hybrid_shortconv_yoco_diffattn_convffn_deepnorm'


def causal_dwconv(x, w):
    s = x.shape[1]
    xp = jnp.pad(x, ((0, 0), (CONV_WIDTH - 1, 0), (0, 0)))
    y = xp[:, 0:s] * w[0]
    for j in range(1, CONV_WIDTH):
        y = y + xp[:, j:j + s] * w[j]
    return y


def layer_norm(x, g, b):
    xf = x.astype(jnp.float32)
    mu = jnp.mean(xf, axis=-1, keepdims=True)
    var = jnp.mean(jnp.square(xf - mu), axis=-1, keepdims=True)
    y = (xf - mu) * lax.rsqrt(var + LN_EPS)
    return (y * g.astype(jnp.float32) + b.astype(jnp.float32)).astype(x.dtype)


def rms_norm(x, g):
    xf = x.astype(jnp.float32)
    y = xf * lax.rsqrt(jnp.mean(jnp.square(xf), axis=-1, keepdims=True) + SUBLN_EPS)
    return (y * g.astype(jnp.float32)).astype(x.dtype)


def rope_tables(positions):
    inv_freq = ROPE_THETA ** (-jnp.arange(0, ROT_DIM, 2, dtype=jnp.float32) / ROT_DIM)
    ang = positions.astype(jnp.float32)[..., None] * inv_freq
    return jnp.cos(ang), jnp.sin(ang)


def partial_rope(t, cos, sin):
    half = ROT_DIM // 2
    rot = t[..., :ROT_DIM].astype(jnp.float32)
    x1, x2 = rot[..., :half], rot[..., half:]
    c = cos[:, :, None, None, :]
    s = sin[:, :, None, None, :]
    rotated = jnp.concatenate([x1 * c - x2 * s, x2 * c + x1 * s], axis=-1).astype(t.dtype)
    return jnp.concatenate([rotated, t[..., ROT_DIM:]], axis=-1)


def short_conv_mixer(x, w_in, conv_w, w_out):
    b_gate, c_gate, xv = jnp.split(x @ w_in, 3, axis=-1)
    y = b_gate * causal_dwconv(c_gate * xv, conv_w)
    return y @ w_out


def conv_ffn(x, w_up, conv_w, conv_b, w_down):
    h = causal_dwconv(x @ w_up, conv_w) + conv_b
    g, u = jnp.split(h, 2, axis=-1)
    return (jax.nn.silu(g) * u) @ w_down


def shared_kv(x, w_k, w_v, cos, sin):
    bsz, s, _ = x.shape
    k = partial_rope((x @ w_k).reshape(bsz, s, DIFF_HEADS, 2, DIFF_HEAD_DIM), cos, sin)
    v = (x @ w_v).reshape(bsz, s, DIFF_HEADS, V_HEAD_DIM)
    return k, v


def diff_attention(x, k, v, cos, sin, w_q, lam, subln_g, w_o, lambda_init):
    bsz, s, _ = x.shape
    q = (x @ w_q).reshape(bsz, s, DIFF_HEADS, 2, DIFF_HEAD_DIM)
    q = partial_rope(q, cos, sin) * (DIFF_HEAD_DIM ** -0.5)
    lamf = lam.astype(jnp.float32)
    lam_full = (jnp.exp(jnp.sum(lamf[0] * lamf[1])) - jnp.exp(jnp.sum(lamf[2] * lamf[3]))
                + lambda_init)
    kpos = jnp.arange(s)
    neg = jnp.finfo(jnp.float32).min

    def block(i):
        start = i * Q_BLOCK
        qb = lax.dynamic_slice_in_dim(q, start, Q_BLOCK, axis=1)
        sc = jnp.einsum('bqhcd,bkhcd->bhcqk', qb, k).astype(jnp.float32)
        qpos = start + jnp.arange(Q_BLOCK)
        mask = kpos[None, :] <= qpos[:, None]
        p = jax.nn.softmax(jnp.where(mask, sc, neg), axis=-1)
        a = p[:, :, 0] - lam_full * p[:, :, 1]
        return jnp.einsum('bhqk,bkhe->bqhe', a.astype(v.dtype), v)

    o = lax.map(block, jnp.arange(s // Q_BLOCK))
    o = jnp.moveaxis(o, 0, 1).reshape(bsz, s, DIFF_HEADS, V_HEAD_DIM)
    o = rms_norm(o, subln_g) * (1.0 - lambda_init)
    return o.reshape(bsz, s, V_WIDTH) @ w_o


def setup_inputs(seed: int = 0) -> dict:
    key = jax.random.key(seed)
    ks = jax.random.split(key, 16)
    f32 = jnp.float32
    nrm = lambda k, shape: jax.random.normal(k, shape, dtype=f32)
    x = nrm(ks[0], (BATCH, SEQ, D_MODEL))
    positions = jnp.broadcast_to(jnp.arange(SEQ, dtype=jnp.int32), (BATCH, SEQ))
    ln_g = 1.0 + 0.02 * nrm(ks[1], (DEPTH, 2, D_MODEL))
    ln_b = 0.02 * nrm(ks[2], (DEPTH, 2, D_MODEL))
    a_w_in = nrm(ks[3], (N_A_LAYERS, D_MODEL, 3 * D_MODEL)) * D_MODEL ** -0.5
    a_conv_w = nrm(ks[4], (N_A_LAYERS, CONV_WIDTH, D_MODEL)) * CONV_WIDTH ** -0.5
    a_w_out = nrm(ks[5], (N_A_LAYERS, D_MODEL, D_MODEL)) * (D_MODEL ** -0.5 * DEEPNORM_BETA)
    kv_w_k = nrm(ks[6], (D_MODEL, QK_WIDTH)) * D_MODEL ** -0.5
    kv_w_v = nrm(ks[7], (D_MODEL, V_WIDTH)) * (D_MODEL ** -0.5 * DEEPNORM_BETA)
    b_w_q = nrm(ks[8], (N_B_LAYERS, D_MODEL, QK_WIDTH)) * D_MODEL ** -0.5
    b_lambda = 0.1 * nrm(ks[9], (N_B_LAYERS, 4, DIFF_HEAD_DIM))
    b_subln_g = 1.0 + 0.02 * nrm(ks[10], (N_B_LAYERS, V_HEAD_DIM))
    b_w_o = nrm(ks[11], (N_B_LAYERS, V_WIDTH, D_MODEL)) * (V_WIDTH ** -0.5 * DEEPNORM_BETA)
    ffn_w_up = nrm(ks[12], (DEPTH, D_MODEL, 2 * D_FF)) * D_MODEL ** -0.5
    ffn_conv_w = nrm(ks[13], (DEPTH, CONV_WIDTH, 2 * D_FF)) * CONV_WIDTH ** -0.5
    ffn_conv_b = 0.02 * nrm(ks[14], (DEPTH, 2 * D_FF))
    ffn_w_down = nrm(ks[15], (DEPTH, D_FF, D_MODEL)) * (D_FF ** -0.5 * DEEPNORM_BETA)
    return {'x': x, 'positions': positions, 'ln_g': ln_g, 'ln_b': ln_b,
            'a_w_in': a_w_in, 'a_conv_w': a_conv_w, 'a_w_out': a_w_out,
            'kv_w_k': kv_w_k, 'kv_w_v': kv_w_v,
            'b_w_q': b_w_q, 'b_lambda': b_lambda, 'b_subln_g': b_subln_g, 'b_w_o': b_w_o,
            'ffn_w_up': ffn_w_up, 'ffn_conv_w': ffn_conv_w, 'ffn_conv_b': ffn_conv_b,
            'ffn_w_down': ffn_w_down}


def reference(x, positions, ln_g, ln_b, a_w_in, a_conv_w, a_w_out, kv_w_k, kv_w_v,
              b_w_q, b_lambda, b_subln_g, b_w_o, ffn_w_up, ffn_conv_w, ffn_conv_b,
              ffn_w_down):
    cos, sin = rope_tables(positions)
    k_sh, v_sh = None, None
    for layer in range(DEPTH):
        if layer < N_A_LAYERS:
            mix = short_conv_mixer(x, a_w_in[layer], a_conv_w[layer], a_w_out[layer])
        else:
            j = layer - N_A_LAYERS
            if j == 0:
                k_sh, v_sh = shared_kv(x, kv_w_k, kv_w_v, cos, sin)
            lambda_init = 0.8 - 0.6 * math.exp(-0.3 * layer)
            mix = diff_attention(x, k_sh, v_sh, cos, sin, b_w_q[j], b_lambda[j],
                                 b_subln_g[j], b_w_o[j], lambda_init)
        x = layer_norm(DEEPNORM_ALPHA * x + mix, ln_g[layer, 0], ln_b[layer, 0])
        ffn = conv_ffn(x, ffn_w_up[layer], ffn_conv_w[layer], ffn_conv_b[layer], ffn_w_down[layer])
        x = layer_norm(DEEPNORM_ALPHA * x + ffn, ln_g[layer, 1], ln_b[layer, 1])
    return x
```

```python
import functools
import math

import jax
import jax.numpy as jnp
from jax import lax
from jax.experimental import pallas as pl
from jax.experimental.pallas import tpu as pltpu

CONV_WIDTH = 3
HEAD_WIDTH = 256
ROPE_THETA = 500000.0
LN_EPS = 1e-5
SUBLN_EPS = 1e-5

V7X_LANES = 128
V7X_SUBLANES = 8
V7X_VMEM_BYTES = 64 * 1024 * 1024
VMEM_LIMIT_BYTES = V7X_VMEM_BYTES - 8 * 1024 * 1024

BF16 = jnp.bfloat16
F32 = jnp.float32


def _compiler_params(semantics):
    return pltpu.CompilerParams(dimension_semantics=semantics, vmem_limit_bytes=VMEM_LIMIT_BYTES)


def _dot(a, b):
    return jnp.dot(a, b, preferred_element_type=F32)


def _layer_norm_rows(z, g, b):
    mu = jnp.mean(z, axis=-1, keepdims=True)
    zc = z - mu
    var = jnp.mean(zc * zc, axis=-1, keepdims=True)
    return zc * lax.rsqrt(var + LN_EPS) * g + b


def _causal_conv3(s_ref, rows, cw):
    pad = V7X_SUBLANES
    return (cw[2:3, :] * s_ref[pad:pad + rows, :]
            + cw[1:2, :] * s_ref[pad - 1:pad - 1 + rows, :]
            + cw[0:1, :] * s_ref[pad - 2:pad - 2 + rows, :])


def _rope_table_kernel(pos_ref, freq_ref, cos_ref, sin_ref):
    ang = pos_ref[...] * freq_ref[...]
    cos_ref[...] = jnp.cos(ang)
    sin_ref[...] = jnp.sin(ang)


def _rope_tables(positions, rot_dim):
    half = rot_dim // 2
    tokens = positions.size
    per_row = V7X_LANES // half
    inv_freq = ROPE_THETA ** (-jnp.arange(0, rot_dim, 2, dtype=F32) / rot_dim)
    pos = jnp.repeat(positions.astype(F32).reshape(tokens // per_row, per_row), half, axis=1)
    freq = jnp.tile(inv_freq, per_row).reshape(1, V7X_LANES)
    shape = jax.ShapeDtypeStruct(pos.shape, F32)
    cos, sin = pl.pallas_call(_rope_table_kernel, out_shape=(shape, shape), name="rope_tables")(pos, freq)
    cos = cos.reshape(tokens, half)
    sin = sin.reshape(tokens, half)
    rest = V7X_LANES - rot_dim
    c_tab = jnp.concatenate([cos, cos, jnp.ones((tokens, rest), F32)], axis=1)
    s_tab = jnp.concatenate([-sin, sin, jnp.zeros((tokens, rest), F32)], axis=1)
    return c_tab, s_tab


def _mixer_in_kernel(x_ref, wb_ref, wc_ref, wv_ref, cw_ref, o_ref, s_ref, *, tiles_per_seq):
    i = pl.program_id(1)
    rows = x_ref.shape[0]
    pad = V7X_SUBLANES

    @pl.when(i % tiles_per_seq == 0)
    def _():
        s_ref[0:pad, :] = jnp.zeros((pad, s_ref.shape[1]), F32)

    xb = x_ref[...].astype(BF16)
    s_ref[pad:pad + rows, :] = _dot(xb, wc_ref[...]) * _dot(xb, wv_ref[...])
    y = _causal_conv3(s_ref, rows, cw_ref[...])
    o_ref[...] = (_dot(xb, wb_ref[...]) * y).astype(o_ref.dtype)
    s_ref[0:pad, :] = s_ref[rows:rows + pad, :]


def _mixer_in(x, w_in, conv_w, *, seq, tm, tn):
    tokens, d = x.shape
    nj = d // tn
    kern = functools.partial(_mixer_in_kernel, tiles_per_seq=seq // tm)
    return pl.pallas_call(
        kern,
        grid=(nj, tokens // tm),
        in_specs=[
            pl.BlockSpec((tm, d), lambda j, i: (i, 0)),
            pl.BlockSpec((d, tn), lambda j, i: (0, j)),
            pl.BlockSpec((d, tn), lambda j, i: (0, j + nj)),
            pl.BlockSpec((d, tn), lambda j, i: (0, j + 2 * nj)),
            pl.BlockSpec((CONV_WIDTH, tn), lambda j, i: (0, j)),
        ],
        out_specs=pl.BlockSpec((tm, tn), lambda j, i: (i, j)),
        out_shape=jax.ShapeDtypeStruct((tokens, d), BF16),
        scratch_shapes=[pltpu.VMEM((tm + V7X_SUBLANES, tn), F32)],
        compiler_params=_compiler_params(("arbitrary", "arbitrary")),
        name="mixer_in",
    )(x, w_in, w_in, w_in, conv_w)


def _proj_ln_kernel(h_ref, w_ref, r_ref, g_ref, b_ref, o_ref, *, alpha):
    z = alpha * r_ref[...] + _dot(h_ref[...], w_ref[...])
    o_ref[...] = _layer_norm_rows(z, g_ref[...], b_ref[...])


def _proj_ln(h, w, resid, g, b, *, alpha, tm):
    tokens, d_in = h.shape
    d = w.shape[1]
    kern = functools.partial(_proj_ln_kernel, alpha=alpha)
    return pl.pallas_call(
        kern,
        grid=(tokens // tm,),
        in_specs=[
            pl.BlockSpec((tm, d_in), lambda i: (i, 0)),
            pl.BlockSpec((d_in, d), lambda i: (0, 0)),
            pl.BlockSpec((tm, d), lambda i: (i, 0)),
            pl.BlockSpec((1, d), lambda i: (0, 0)),
            pl.BlockSpec((1, d), lambda i: (0, 0)),
        ],
        out_specs=pl.BlockSpec((tm, d), lambda i: (i, 0)),
        out_shape=jax.ShapeDtypeStruct((tokens, d), F32),
        compiler_params=_compiler_params(("parallel",)),
        name="proj_ln",
    )(h, w, resid, g.reshape(1, d), b.reshape(1, d))


def _ffn_kernel(x_ref, wg_ref, wu_ref, cwg_ref, cwu_ref, cbg_ref, cbu_ref, wd_ref, g_ref, b_ref,
                o_ref, xb_ref, acc_ref, sg_ref, su_ref, hist_ref, *, alpha, tiles_per_seq):
    i = pl.program_id(0)
    j = pl.program_id(1)
    rows = x_ref.shape[0]
    tf = wg_ref.shape[1]
    pad = V7X_SUBLANES

    @pl.when(j == 0)
    def _():
        xb_ref[...] = x_ref[...].astype(BF16)
        acc_ref[...] = jnp.zeros(acc_ref.shape, F32)

    seq_start = i % tiles_per_seq == 0

    @pl.when(seq_start)
    def _():
        sg_ref[0:pad, :] = jnp.zeros((pad, tf), F32)
        su_ref[0:pad, :] = jnp.zeros((pad, tf), F32)

    @pl.when(jnp.logical_not(seq_start))
    def _():
        hist = hist_ref[j]
        sg_ref[0:pad, :] = hist[:, :tf]
        su_ref[0:pad, :] = hist[:, tf:]

    xb = xb_ref[...]
    sg_ref[pad:pad + rows, :] = _dot(xb, wg_ref[...])
    su_ref[pad:pad + rows, :] = _dot(xb, wu_ref[...])
    gate = _causal_conv3(sg_ref, rows, cwg_ref[...]) + cbg_ref[...]
    up = _causal_conv3(su_ref, rows, cwu_ref[...]) + cbu_ref[...]
    act = (gate * jax.nn.sigmoid(gate) * up).astype(BF16)
    acc_ref[...] += _dot(act, wd_ref[...])
    hist_ref[j] = jnp.concatenate([sg_ref[rows:rows + pad, :], su_ref[rows:rows + pad, :]], axis=1)

    @pl.when(j == pl.num_programs(1) - 1)
    def _():
        z = alpha * x_ref[...] + acc_ref[...]
        o_ref[...] = _layer_norm_rows(z, g_ref[...], b_ref[...])


def _ffn(x, w_up, conv_w, conv_b, w_down, g, b, *, alpha, seq, tm, tf):
    tokens, d = x.shape
    d_ff = w_down.shape[0]
    nf = d_ff // tf
    kern = functools.partial(_ffn_kernel, alpha=alpha, tiles_per_seq=seq // tm)
    conv_b = conv_b.reshape(1, 2 * d_ff)
    pad = V7X_SUBLANES
    return pl.pallas_call(
        kern,
        grid=(tokens // tm, nf),
        in_specs=[
            pl.BlockSpec((tm, d), lambda i, j: (i, 0)),
            pl.BlockSpec((d, tf), lambda i, j: (0, j)),
            pl.BlockSpec((d, tf), lambda i, j: (0, j + nf)),
            pl.BlockSpec((CONV_WIDTH, tf), lambda i, j: (0, j)),
            pl.BlockSpec((CONV_WIDTH, tf), lambda i, j: (0, j + nf)),
            pl.BlockSpec((1, tf), lambda i, j: (0, j)),
            pl.BlockSpec((1, tf), lambda i, j: (0, j + nf)),
            pl.BlockSpec((tf, d), lambda i, j: (j, 0)),
            pl.BlockSpec((1, d), lambda i, j: (0, 0)),
            pl.BlockSpec((1, d), lambda i, j: (0, 0)),
        ],
        out_specs=pl.BlockSpec((tm, d), lambda i, j: (i, 0)),
        out_shape=jax.ShapeDtypeStruct((tokens, d), F32),
        scratch_shapes=[
            pltpu.VMEM((tm, d), BF16),
            pltpu.VMEM((tm, d), F32),
            pltpu.VMEM((tm + pad, tf), F32),
            pltpu.VMEM((tm + pad, tf), F32),
            pltpu.VMEM((nf, pad, 2 * tf), F32),
        ],
        compiler_params=_compiler_params(("arbitrary", "arbitrary")),
        name="conv_ffn",
    )(x, w_up, w_up, conv_w, conv_w, conv_b, conv_b, w_down, g.reshape(1, d), b.reshape(1, d))


def _qkv_rope_kernel(x_ref, w_ref, c_ref, s_ref, o_ref, *, q_scale, rot_dim):
    p = pl.program_id(0)
    y = _dot(x_ref[...].astype(BF16), w_ref[...])

    @pl.when(p >= 2)
    def _():
        o_ref[...] = y.astype(o_ref.dtype)

    @pl.when(p < 2)
    def _():
        scale = jnp.where(p == 0, q_scale, 1.0).astype(F32)
        c_tab = c_ref[...]
        s_tab = s_ref[...]
        lane = lax.broadcasted_iota(jnp.int32, c_tab.shape, 1)
        first_half = lane < rot_dim // 2
        for c in range(y.shape[1] // V7X_LANES):
            t = y[:, c * V7X_LANES:(c + 1) * V7X_LANES]
            partner = jnp.where(first_half,
                                pltpu.roll(t, V7X_LANES - rot_dim // 2, 1),
                                pltpu.roll(t, rot_dim // 2, 1))
            o_ref[:, c * V7X_LANES:(c + 1) * V7X_LANES] = (
                (t * c_tab + partner * s_tab) * scale).astype(o_ref.dtype)


def _qkv_rope(x, w_stack, c_tab, s_tab, *, q_scale, rot_dim, tm):
    tokens, d = x.shape
    n = w_stack.shape[0]
    kern = functools.partial(_qkv_rope_kernel, q_scale=q_scale, rot_dim=rot_dim)
    return pl.pallas_call(
        kern,
        grid=(n, tokens // tm),
        in_specs=[
            pl.BlockSpec((tm, d), lambda p, i: (i, 0)),
            pl.BlockSpec((None, d, d), lambda p, i: (p, 0, 0)),
            pl.BlockSpec((tm, V7X_LANES), lambda p, i: (i, 0)),
            pl.BlockSpec((tm, V7X_LANES), lambda p, i: (i, 0)),
        ],
        out_specs=pl.BlockSpec((None, tm, d), lambda p, i: (p, i, 0)),
        out_shape=jax.ShapeDtypeStruct((n, tokens, d), BF16),
        compiler_params=_compiler_params(("parallel", "parallel")),
        name="qkv_rope",
    )(x, w_stack, c_tab, s_tab)


def _diff_attn_kernel(q_ref, k_ref, v_ref, lam_ref, g_ref, o_ref, m_ref, l_ref, acc_ref,
                      *, lambda_init, tk):
    qi = pl.program_id(2)
    tq = q_ref.shape[0]
    dh = q_ref.shape[1] // 2
    reps = tk // V7X_LANES
    neg = jnp.finfo(F32).min

    m_ref[...] = jnp.full(m_ref.shape, neg, F32)
    l_ref[...] = jnp.zeros(l_ref.shape, F32)
    acc_ref[...] = jnp.zeros(acc_ref.shape, F32)
    q = q_ref[...]

    def block(j, masked):
        start = pl.multiple_of(j * tk, tk)
        ks = k_ref[pl.ds(start, tk), :]
        vs = v_ref[pl.ds(start, tk), :]
        for c in range(2):
            s = lax.dot_general(q[:, c * dh:(c + 1) * dh], ks[:, c * dh:(c + 1) * dh],
                                (((1,), (1,)), ((), ())), preferred_element_type=F32)
            if masked:
                row = lax.broadcasted_iota(jnp.int32, s.shape, 0)
                col = lax.broadcasted_iota(jnp.int32, s.shape, 1)
                s = jnp.where(col <= row, s, neg)
            m_prev = m_ref[c]
            m_next = jnp.maximum(m_prev, jnp.max(s, axis=1, keepdims=True))
            alpha = jnp.exp(m_prev - m_next)
            p = jnp.exp(s - jnp.concatenate([m_next] * reps, axis=1))
            l_ref[c] = alpha * l_ref[c] + jnp.sum(p, axis=1, keepdims=True)
            m_ref[c] = m_next
            acc_ref[c] = (acc_ref[c] * jnp.concatenate([alpha] * (2 * dh // V7X_LANES), axis=1)
                          + _dot(p.astype(BF16), vs))

    def body(j, carry):
        block(j, masked=False)
        return carry

    lax.fori_loop(0, qi, body, 0)
    block(qi, masked=True)

    lam = lam_ref[...]
    lam_full = (jnp.exp(jnp.sum(lam[0:1] * lam[1:2], axis=1, keepdims=True))
                - jnp.exp(jnp.sum(lam[2:3] * lam[3:4], axis=1, keepdims=True)) + lambda_init)
    inv = [jnp.concatenate([1.0 / l_ref[c]] * (2 * dh // V7X_LANES), axis=1) for c in range(2)]
    o = acc_ref[0] * inv[0] - lam_full * (acc_ref[1] * inv[1])
    ms = jnp.mean(o * o, axis=-1, keepdims=True)
    o_ref[...] = (o * lax.rsqrt(ms + SUBLN_EPS) * g_ref[...] * (1.0 - lambda_init)).astype(o_ref.dtype)


def _diff_attn(q_src, kv_src, lam, subln_g, *, batch, seq, lambda_init, tq):
    tokens, d = q_src.shape[1:]
    heads = d // HEAD_WIDTH
    nq = seq // tq
    kern = functools.partial(_diff_attn_kernel, lambda_init=lambda_init, tk=tq)
    return pl.pallas_call(
        kern,
        grid=(batch, heads, nq),
        in_specs=[
            pl.BlockSpec((None, tq, HEAD_WIDTH), lambda b, h, i: (0, b * nq + i, h)),
            pl.BlockSpec((None, seq, HEAD_WIDTH), lambda b, h, i: (1, b, h)),
            pl.BlockSpec((None, seq, HEAD_WIDTH), lambda b, h, i: (2, b, h)),
            pl.BlockSpec(lam.shape, lambda b, h, i: (0, 0)),
            pl.BlockSpec((1, HEAD_WIDTH), lambda b, h, i: (0, 0)),
        ],
        out_specs=pl.BlockSpec((tq, HEAD_WIDTH), lambda b, h, i: (b * nq + i, h)),
        out_shape=jax.ShapeDtypeStruct((tokens, d), BF16),
        scratch_shapes=[
            pltpu.VMEM((2, tq, V7X_LANES), F32),
            pltpu.VMEM((2, tq, V7X_LANES), F32),
            pltpu.VMEM((2, tq, HEAD_WIDTH), F32),
        ],
        compiler_params=_compiler_params(("parallel", "parallel", "arbitrary")),
        name="diff_attn",
    )(q_src, kv_src, kv_src, lam, subln_g.reshape(1, HEAD_WIDTH))


def kernel(x, positions, ln_g, ln_b, a_w_in, a_conv_w, a_w_out, kv_w_k, kv_w_v, b_w_q, b_lambda,
           b_subln_g, b_w_o, ffn_w_up, ffn_conv_w, ffn_conv_b, ffn_w_down):
    batch, seq, d = x.shape
    depth = ln_g.shape[0]
    n_a = a_w_in.shape[0]
    head_dim = HEAD_WIDTH // 2
    rot_dim = head_dim // 4
    alpha = (2.0 * depth) ** 0.25
    tokens = batch * seq

    xs = x.reshape(tokens, d)
    c_tab, s_tab = _rope_tables(positions, rot_dim)
    kv_src = None
    for layer in range(depth):
        if layer < n_a:
            mix = _mixer_in(xs, a_w_in[layer].astype(BF16), a_conv_w[layer], seq=seq, tm=1024, tn=512)
            w_proj = a_w_out[layer]
        else:
            j = layer - n_a
            if j == 0:
                w_stack = jnp.stack([b_w_q[j], kv_w_k, kv_w_v]).astype(BF16)
            else:
                w_stack = b_w_q[j][None].astype(BF16)
            q_src = _qkv_rope(xs, w_stack, c_tab, s_tab, q_scale=head_dim ** -0.5, rot_dim=rot_dim, tm=512)
            if j == 0:
                kv_src = q_src
            lambda_init = 0.8 - 0.6 * math.exp(-0.3 * layer)
            mix = _diff_attn(q_src, kv_src, b_lambda[j], b_subln_g[j], batch=batch, seq=seq,
                             lambda_init=lambda_init, tq=512)
            w_proj = b_w_o[j]
        xs = _proj_ln(mix, w_proj.astype(BF16), xs, ln_g[layer, 0], ln_b[layer, 0], alpha=alpha, tm=512)
        xs = _ffn(xs, ffn_w_up[layer].astype(BF16), ffn_conv_w[layer], ffn_conv_b[layer],
                  ffn_w_down[layer].astype(BF16), ln_g[layer, 1], ln_b[layer, 1],
                  alpha=alpha, seq=seq, tm=512, tf=512)
    return xs.reshape(batch, seq, d)
```

```python
import functools
import math

import jax
import jax.numpy as jnp
from jax import lax
from jax.experimental import pallas as pl
from jax.experimental.pallas import tpu as pltpu

CONV_WIDTH = 3
HEAD_WIDTH = 256
ROPE_THETA = 500000.0
LN_EPS = 1e-5
SUBLN_EPS = 1e-5
LOG2_E = math.log2(math.e)

V7X_LANES = 128
V7X_SUBLANES = 8
V7X_VMEM_BYTES = 64 * 1024 * 1024
VMEM_LIMIT_BYTES = V7X_VMEM_BYTES - 8 * 1024 * 1024

TILES = dict(
    mixer_rows=1024, mixer_cols=512,
    proj_rows=512, proj_row_chunks=4,
    ffn_rows=512, ffn_cols=512,
    qkv_rows=512, qkv_col_chunk=512,
    attn_rows=512, attn_heads=2,
)

BF16 = jnp.bfloat16
F32 = jnp.float32


def _compiler_params(semantics):
    return pltpu.CompilerParams(dimension_semantics=semantics, vmem_limit_bytes=VMEM_LIMIT_BYTES)


def _dot(a, b):
    return jnp.dot(a, b, preferred_element_type=F32)


def _layer_norm_rows(z, g, b):
    mu = jnp.mean(z, axis=-1, keepdims=True)
    zc = z - mu
    var = jnp.mean(zc * zc, axis=-1, keepdims=True)
    return zc * lax.rsqrt(var + LN_EPS) * g + b


def _causal_conv3(s_ref, rows, cw):
    pad = V7X_SUBLANES
    return (cw[2:3, :] * s_ref[pad:pad + rows, :]
            + cw[1:2, :] * s_ref[pad - 1:pad - 1 + rows, :]
            + cw[0:1, :] * s_ref[pad - 2:pad - 2 + rows, :])


def _rope_table_kernel(pos_ref, freq_ref, cos_ref, sin_ref):
    ang = pos_ref[...] * freq_ref[...]
    cos_ref[...] = jnp.cos(ang)
    sin_ref[...] = jnp.sin(ang)


def _rope_tables(positions, rot_dim):
    half = rot_dim // 2
    tokens = positions.size
    per_row = V7X_LANES // half
    inv_freq = ROPE_THETA ** (-jnp.arange(0, rot_dim, 2, dtype=F32) / rot_dim)
    pos = jnp.repeat(positions.astype(F32).reshape(tokens // per_row, per_row), half, axis=1)
    freq = jnp.tile(inv_freq, per_row).reshape(1, V7X_LANES)
    shape = jax.ShapeDtypeStruct(pos.shape, F32)
    cos, sin = pl.pallas_call(_rope_table_kernel, out_shape=(shape, shape), name="rope_tables")(pos, freq)
    cos = cos.reshape(tokens, half)
    sin = sin.reshape(tokens, half)
    rest = V7X_LANES - rot_dim
    c_tab = jnp.concatenate([cos, cos, jnp.ones((tokens, rest), F32)], axis=1)
    s_tab = jnp.concatenate([-sin, sin, jnp.zeros((tokens, rest), F32)], axis=1)
    return c_tab, s_tab


def _mixer_in_kernel(x_ref, wb_ref, wc_ref, wv_ref, cw_ref, o_ref, s_ref, *, tiles_per_seq):
    i = pl.program_id(1)
    rows = x_ref.shape[0]
    pad = V7X_SUBLANES

    @pl.when(i % tiles_per_seq == 0)
    def _():
        s_ref[0:pad, :] = jnp.zeros((pad, s_ref.shape[1]), F32)

    xb = x_ref[...].astype(BF16)
    s_ref[pad:pad + rows, :] = _dot(xb, wc_ref[...]) * _dot(xb, wv_ref[...])
    y = _causal_conv3(s_ref, rows, cw_ref[...])
    o_ref[...] = (_dot(xb, wb_ref[...]) * y).astype(o_ref.dtype)
    s_ref[0:pad, :] = s_ref[rows:rows + pad, :]


def _mixer_in(x, w_in, conv_w, layer, *, seq):
    tokens, d = x.shape
    tm, tn = TILES["mixer_rows"], TILES["mixer_cols"]
    nj = d // tn
    kern = functools.partial(_mixer_in_kernel, tiles_per_seq=seq // tm)
    return pl.pallas_call(
        kern,
        grid=(nj, tokens // tm),
        in_specs=[
            pl.BlockSpec((tm, d), lambda j, i: (i, 0)),
            pl.BlockSpec((None, d, tn), lambda j, i: (layer, 0, j)),
            pl.BlockSpec((None, d, tn), lambda j, i: (layer, 0, j + nj)),
            pl.BlockSpec((None, d, tn), lambda j, i: (layer, 0, j + 2 * nj)),
            pl.BlockSpec((None, CONV_WIDTH, tn), lambda j, i: (layer, 0, j)),
        ],
        out_specs=pl.BlockSpec((tm, tn), lambda j, i: (i, j)),
        out_shape=jax.ShapeDtypeStruct((tokens, d), BF16),
        scratch_shapes=[pltpu.VMEM((tm + V7X_SUBLANES, tn), F32)],
        compiler_params=_compiler_params(("arbitrary", "arbitrary")),
        name="mixer_in",
    )(x, w_in, w_in, w_in, conv_w)


def _proj_ln_kernel(h_ref, w_ref, r_ref, g_ref, b_ref, o_ref, *, alpha, row_chunks):
    rows = h_ref.shape[0] // row_chunks
    for c in range(row_chunks):
        sl = slice(c * rows, (c + 1) * rows)
        z = alpha * r_ref[sl, :] + _dot(h_ref[sl, :], w_ref[...])
        o_ref[sl, :] = _layer_norm_rows(z, g_ref[...], b_ref[...])


def _proj_ln(h, w, w_idx, resid, ln_g, ln_b, ln_idx, *, alpha):
    tokens, d_in = h.shape
    d = w.shape[2]
    tm = TILES["proj_rows"]
    kern = functools.partial(_proj_ln_kernel, alpha=alpha, row_chunks=TILES["proj_row_chunks"])
    return pl.pallas_call(
        kern,
        grid=(tokens // tm,),
        in_specs=[
            pl.BlockSpec((tm, d_in), lambda i: (i, 0)),
            pl.BlockSpec((None, d_in, d), lambda i: (w_idx, 0, 0)),
            pl.BlockSpec((tm, d), lambda i: (i, 0)),
            pl.BlockSpec((None, 1, d), lambda i: (ln_idx, 0, 0)),
            pl.BlockSpec((None, 1, d), lambda i: (ln_idx, 0, 0)),
        ],
        out_specs=pl.BlockSpec((tm, d), lambda i: (i, 0)),
        out_shape=jax.ShapeDtypeStruct((tokens, d), F32),
        compiler_params=_compiler_params(("parallel",)),
        name="proj_ln",
    )(h, w, resid, ln_g, ln_b)


def _ffn_kernel(x_ref, wg_ref, wu_ref, cwg_ref, cwu_ref, cbg_ref, cbu_ref, wd_ref, g_ref, b_ref,
                o_ref, xb_ref, sg_ref, su_ref, hist_ref, *, alpha, tiles_per_seq):
    i = pl.program_id(0)
    j = pl.program_id(1)
    rows = x_ref.shape[0]
    tf = wg_ref.shape[1]
    pad = V7X_SUBLANES

    @pl.when(j == 0)
    def _():
        xb_ref[...] = x_ref[...].astype(BF16)
        o_ref[...] = alpha * x_ref[...]

    seq_start = i % tiles_per_seq == 0

    @pl.when(seq_start)
    def _():
        sg_ref[0:pad, :] = jnp.zeros((pad, tf), F32)
        su_ref[0:pad, :] = jnp.zeros((pad, tf), F32)

    @pl.when(jnp.logical_not(seq_start))
    def _():
        hist = hist_ref[j]
        sg_ref[0:pad, :] = hist[:, :tf]
        su_ref[0:pad, :] = hist[:, tf:]

    xb = xb_ref[...]
    sg_ref[pad:pad + rows, :] = _dot(xb, wg_ref[...])
    su_ref[pad:pad + rows, :] = _dot(xb, wu_ref[...])
    gate = _causal_conv3(sg_ref, rows, cwg_ref[...]) + cbg_ref[...]
    up = _causal_conv3(su_ref, rows, cwu_ref[...]) + cbu_ref[...]
    act = (gate * jax.nn.sigmoid(gate) * up).astype(BF16)
    o_ref[...] += _dot(act, wd_ref[...])
    hist_ref[j] = jnp.concatenate([sg_ref[rows:rows + pad, :], su_ref[rows:rows + pad, :]], axis=1)

    @pl.when(j == pl.num_programs(1) - 1)
    def _():
        o_ref[...] = _layer_norm_rows(o_ref[...], g_ref[...], b_ref[...])


def _ffn(x, w_up, conv_w, conv_b, w_down, layer, ln_g, ln_b, ln_idx, *, alpha, seq):
    tokens, d = x.shape
    d_ff = w_down.shape[1]
    tm, tf = TILES["ffn_rows"], TILES["ffn_cols"]
    nf = d_ff // tf
    kern = functools.partial(_ffn_kernel, alpha=alpha, tiles_per_seq=seq // tm)
    pad = V7X_SUBLANES
    return pl.pallas_call(
        kern,
        grid=(tokens // tm, nf),
        in_specs=[
            pl.BlockSpec((tm, d), lambda i, j: (i, 0)),
            pl.BlockSpec((None, d, tf), lambda i, j: (layer, 0, j)),
            pl.BlockSpec((None, d, tf), lambda i, j: (layer, 0, j + nf)),
            pl.BlockSpec((None, CONV_WIDTH, tf), lambda i, j: (layer, 0, j)),
            pl.BlockSpec((None, CONV_WIDTH, tf), lambda i, j: (layer, 0, j + nf)),
            pl.BlockSpec((None, 1, tf), lambda i, j: (layer, 0, j)),
            pl.BlockSpec((None, 1, tf), lambda i, j: (layer, 0, j + nf)),
            pl.BlockSpec((None, tf, d), lambda i, j: (layer, j, 0)),
            pl.BlockSpec((None, 1, d), lambda i, j: (ln_idx, 0, 0)),
            pl.BlockSpec((None, 1, d), lambda i, j: (ln_idx, 0, 0)),
        ],
        out_specs=pl.BlockSpec((tm, d), lambda i, j: (i, 0)),
        out_shape=jax.ShapeDtypeStruct((tokens, d), F32),
        scratch_shapes=[
            pltpu.VMEM((tm, d), BF16),
            pltpu.VMEM((tm + pad, tf), F32),
            pltpu.VMEM((tm + pad, tf), F32),
            pltpu.VMEM((nf, pad, 2 * tf), F32),
        ],
        compiler_params=_compiler_params(("arbitrary", "arbitrary")),
        name="conv_ffn",
    )(x, w_up, w_up, conv_w, conv_w, conv_b, conv_b, w_down, ln_g, ln_b)


def _qkv_rope_kernel(x_ref, w_ref, c_ref, s_ref, o_ref, *, q_scale, rot_dim, chunk):
    p = pl.program_id(0)
    rope = p < 2
    scale = jnp.where(p == 0, q_scale, 1.0).astype(F32)
    c_tab = jnp.where(rope, c_ref[...], 1.0) * scale
    s_tab = jnp.where(rope, s_ref[...], 0.0) * scale
    lane = lax.broadcasted_iota(jnp.int32, c_tab.shape, 1)
    first_half = lane < rot_dim // 2
    xb = x_ref[...].astype(BF16)
    for c0 in range(0, w_ref.shape[1], chunk):
        y = _dot(xb, w_ref[:, c0:c0 + chunk])
        for c in range(0, chunk, V7X_LANES):
            t = y[:, c:c + V7X_LANES]
            partner = jnp.where(first_half,
                                pltpu.roll(t, V7X_LANES - rot_dim // 2, 1),
                                pltpu.roll(t, rot_dim // 2, 1))
            o_ref[:, c0 + c:c0 + c + V7X_LANES] = (t * c_tab + partner * s_tab).astype(o_ref.dtype)


def _qkv_rope(x, w_stack, c_tab, s_tab, *, q_scale, rot_dim):
    tokens, d = x.shape
    n = w_stack.shape[0]
    tm = TILES["qkv_rows"]
    kern = functools.partial(_qkv_rope_kernel, q_scale=q_scale, rot_dim=rot_dim, chunk=TILES["qkv_col_chunk"])
    return pl.pallas_call(
        kern,
        grid=(n, tokens // tm),
        in_specs=[
            pl.BlockSpec((tm, d), lambda p, i: (i, 0)),
            pl.BlockSpec((None, d, d), lambda p, i: (p, 0, 0)),
            pl.BlockSpec((tm, V7X_LANES), lambda p, i: (i, 0)),
            pl.BlockSpec((tm, V7X_LANES), lambda p, i: (i, 0)),
        ],
        out_specs=pl.BlockSpec((None, tm, d), lambda p, i: (p, i, 0)),
        out_shape=jax.ShapeDtypeStruct((n, tokens, d), BF16),
        compiler_params=_compiler_params(("parallel", "parallel")),
        name="qkv_rope",
    )(x, w_stack, c_tab, s_tab)


def _diff_attn_kernel(q_ref, k_ref, v_ref, lam_ref, g_ref, o_ref, m_ref, l_ref, acc_ref,
                      *, lambda_init, tk):
    qi = pl.program_id(2)
    dh = HEAD_WIDTH // 2
    chains = q_ref.shape[1] // dh
    reps = tk // V7X_LANES
    neg = jnp.finfo(F32).min

    m_ref[...] = jnp.full(m_ref.shape, neg, F32)
    l_ref[...] = jnp.zeros(l_ref.shape, F32)
    acc_ref[...] = jnp.zeros(acc_ref.shape, F32)

    def block(j, masked):
        start = pl.multiple_of(j * tk, tk)
        for n in range(chains):
            hv = (n // 2) * HEAD_WIDTH
            s = lax.dot_general(q_ref[:, n * dh:(n + 1) * dh], k_ref[pl.ds(start, tk), n * dh:(n + 1) * dh],
                                (((1,), (1,)), ((), ())), preferred_element_type=F32)
            if masked:
                row = lax.broadcasted_iota(jnp.int32, s.shape, 0)
                col = lax.broadcasted_iota(jnp.int32, s.shape, 1)
                s = jnp.where(col <= row, s, neg)
            m_prev = m_ref[n]
            m_next = jnp.maximum(m_prev, jnp.max(s, axis=1, keepdims=True))
            alpha = jnp.exp2(m_prev - m_next)
            p = jnp.exp2(s - jnp.concatenate([m_next] * reps, axis=1))
            l_ref[n] = alpha * l_ref[n] + jnp.sum(p, axis=1, keepdims=True)
            m_ref[n] = m_next
            acc_ref[n] = (acc_ref[n] * jnp.concatenate([alpha] * (HEAD_WIDTH // V7X_LANES), axis=1)
                          + _dot(p.astype(BF16), v_ref[pl.ds(start, tk), hv:hv + HEAD_WIDTH]))

    def pair_body(j2, carry):
        block(2 * j2, masked=False)
        block(2 * j2 + 1, masked=False)
        return carry

    lax.fori_loop(0, qi // 2, pair_body, 0)

    @pl.when(qi % 2 == 1)
    def _():
        block(qi - 1, masked=False)
        block(qi, masked=True)

    @pl.when(qi % 2 == 0)
    def _():
        block(qi, masked=True)

    lam = lam_ref[...]
    lam_full = (jnp.exp(jnp.sum(lam[0:1] * lam[1:2], axis=1, keepdims=True))
                - jnp.exp(jnp.sum(lam[2:3] * lam[3:4], axis=1, keepdims=True)) + lambda_init)
    for h in range(chains // 2):
        inv = [jnp.concatenate([1.0 / l_ref[2 * h + c]] * (HEAD_WIDTH // V7X_LANES), axis=1) for c in range(2)]
        o = acc_ref[2 * h] * inv[0] - lam_full * (acc_ref[2 * h + 1] * inv[1])
        ms = jnp.mean(o * o, axis=-1, keepdims=True)
        o_ref[:, h * HEAD_WIDTH:(h + 1) * HEAD_WIDTH] = (
            o * lax.rsqrt(ms + SUBLN_EPS) * g_ref[...] * (1.0 - lambda_init)).astype(o_ref.dtype)


def _diff_attn(q_src, kv_src, lam, subln_g, idx, *, batch, seq, lambda_init):
    tokens, d = q_src.shape[1:]
    tq = TILES["attn_rows"]
    width = TILES["attn_heads"] * HEAD_WIDTH
    nq = seq // tq
    chains = 2 * TILES["attn_heads"]
    kern = functools.partial(_diff_attn_kernel, lambda_init=lambda_init, tk=tq)
    return pl.pallas_call(
        kern,
        grid=(batch, d // width, nq),
        in_specs=[
            pl.BlockSpec((None, tq, width), lambda b, h, i: (0, b * nq + i, h)),
            pl.BlockSpec((None, seq, width), lambda b, h, i: (1, b, h)),
            pl.BlockSpec((None, seq, width), lambda b, h, i: (2, b, h)),
            pl.BlockSpec((None,) + lam.shape[1:], lambda b, h, i: (idx, 0, 0)),
            pl.BlockSpec((None, 1, HEAD_WIDTH), lambda b, h, i: (idx, 0, 0)),
        ],
        out_specs=pl.BlockSpec((tq, width), lambda b, h, i: (b * nq + i, h)),
        out_shape=jax.ShapeDtypeStruct((tokens, d), BF16),
        scratch_shapes=[
            pltpu.VMEM((chains, tq, V7X_LANES), F32),
            pltpu.VMEM((chains, tq, V7X_LANES), F32),
            pltpu.VMEM((chains, tq, HEAD_WIDTH), F32),
        ],
        compiler_params=_compiler_params(("parallel", "parallel", "arbitrary")),
        name="diff_attn",
    )(q_src, kv_src, kv_src, lam, subln_g)


def kernel(x, positions, ln_g, ln_b, a_w_in, a_conv_w, a_w_out, kv_w_k, kv_w_v, b_w_q, b_lambda,
           b_subln_g, b_w_o, ffn_w_up, ffn_conv_w, ffn_conv_b, ffn_w_down):
    batch, seq, d = x.shape
    depth = ln_g.shape[0]
    n_a = a_w_in.shape[0]
    head_dim = HEAD_WIDTH // 2
    rot_dim = head_dim // 4
    alpha = (2.0 * depth) ** 0.25
    tokens = batch * seq

    a_w_in, a_w_out, b_w_o = a_w_in.astype(BF16), a_w_out.astype(BF16), b_w_o.astype(BF16)
    ffn_w_up, ffn_w_down = ffn_w_up.astype(BF16), ffn_w_down.astype(BF16)
    ln_g = ln_g.reshape(2 * depth, 1, d)
    ln_b = ln_b.reshape(2 * depth, 1, d)
    ffn_conv_b = ffn_conv_b.reshape(depth, 1, -1)
    b_subln_g = b_subln_g.reshape(-1, 1, HEAD_WIDTH)

    xs = x.reshape(tokens, d)
    c_tab, s_tab = _rope_tables(positions, rot_dim)
    kv_src = None
    for layer in range(depth):
        if layer < n_a:
            mix = _mixer_in(xs, a_w_in, a_conv_w, layer, seq=seq)
            w_proj, w_idx = a_w_out, layer
        else:
            j = layer - n_a
            if j == 0:
                w_stack = jnp.stack([b_w_q[j], kv_w_k, kv_w_v]).astype(BF16)
            else:
                w_stack = b_w_q[j][None].astype(BF16)
            q_src = _qkv_rope(xs, w_stack, c_tab, s_tab, q_scale=LOG2_E * head_dim ** -0.5, rot_dim=rot_dim)
            if j == 0:
                kv_src = q_src
            lambda_init = 0.8 - 0.6 * math.exp(-0.3 * layer)
            mix = _diff_attn(q_src, kv_src, b_lambda, b_subln_g, j, batch=batch, seq=seq,
                             lambda_init=lambda_init)
            w_proj, w_idx = b_w_o, j
        xs = _proj_ln(mix, w_proj, w_idx, xs, ln_g, ln_b, 2 * layer, alpha=alpha)
        xs = _ffn(xs, ffn_w_up, ffn_conv_w, ffn_conv_b, ffn_w_down, layer, ln_g, ln_b, 2 * layer + 1,
                  alpha=alpha, seq=seq)
    return xs.reshape(batch, seq, d)
```

```python
import functools
import math

import jax
import jax.numpy as jnp
from jax import lax
from jax.experimental import pallas as pl
from jax.experimental.pallas import tpu as pltpu

CONV_WIDTH = 3
HEAD_WIDTH = 256
ROPE_THETA = 500000.0
LN_EPS = 1e-5
SUBLN_EPS = 1e-5
LOG2_E = math.log2(math.e)

V7X_LANES = 128
V7X_SUBLANES = 8
V7X_VMEM_BYTES = 64 * 1024 * 1024
VMEM_LIMIT_BYTES = V7X_VMEM_BYTES - 8 * 1024 * 1024

TILES = dict(
    mixer_rows=1024, mixer_cols=512,
    proj_rows=512, proj_row_chunks=4,
    ffn_up_rows=1024, ffn_up_cols=512,
    ffn_down_rows=256, ffn_down_row_chunks=2,
    qkv_rows=512, qkv_col_chunk=512,
    attn_rows=512, attn_heads=2,
)

BF16 = jnp.bfloat16
F32 = jnp.float32


def _compiler_params(semantics):
    return pltpu.CompilerParams(dimension_semantics=semantics, vmem_limit_bytes=VMEM_LIMIT_BYTES)


def _dot(a, b):
    return jnp.dot(a, b, preferred_element_type=F32)


def _layer_norm_rows(z, g, b):
    mu = jnp.mean(z, axis=-1, keepdims=True)
    zc = z - mu
    var = jnp.mean(zc * zc, axis=-1, keepdims=True)
    return zc * lax.rsqrt(var + LN_EPS) * g + b


def _causal_conv3(s_ref, rows, cw):
    pad = V7X_SUBLANES
    return (cw[2:3, :] * s_ref[pad:pad + rows, :]
            + cw[1:2, :] * s_ref[pad - 1:pad - 1 + rows, :]
            + cw[0:1, :] * s_ref[pad - 2:pad - 2 + rows, :])


def _rope_table_kernel(pos_ref, freq_ref, cos_ref, sin_ref):
    ang = pos_ref[...] * freq_ref[...]
    cos_ref[...] = jnp.cos(ang)
    sin_ref[...] = jnp.sin(ang)


def _rope_tables(positions, rot_dim):
    half = rot_dim // 2
    tokens = positions.size
    per_row = V7X_LANES // half
    inv_freq = ROPE_THETA ** (-jnp.arange(0, rot_dim, 2, dtype=F32) / rot_dim)
    pos = jnp.repeat(positions.astype(F32).reshape(tokens // per_row, per_row), half, axis=1)
    freq = jnp.tile(inv_freq, per_row).reshape(1, V7X_LANES)
    shape = jax.ShapeDtypeStruct(pos.shape, F32)
    cos, sin = pl.pallas_call(_rope_table_kernel, out_shape=(shape, shape), name="rope_tables")(pos, freq)
    cos = cos.reshape(tokens, half)
    sin = sin.reshape(tokens, half)
    rest = V7X_LANES - rot_dim
    c_tab = jnp.concatenate([cos, cos, jnp.ones((tokens, rest), F32)], axis=1)
    s_tab = jnp.concatenate([-sin, sin, jnp.zeros((tokens, rest), F32)], axis=1)
    return c_tab, s_tab


def _mixer_in_kernel(x_ref, wb_ref, wc_ref, wv_ref, cw_ref, o_ref, s_ref, *, tiles_per_seq):
    i = pl.program_id(1)
    rows = x_ref.shape[0]
    pad = V7X_SUBLANES

    @pl.when(i % tiles_per_seq == 0)
    def _():
        s_ref[0:pad, :] = jnp.zeros((pad, s_ref.shape[1]), F32)

    xb = x_ref[...].astype(BF16)
    s_ref[pad:pad + rows, :] = _dot(xb, wc_ref[...]) * _dot(xb, wv_ref[...])
    y = _causal_conv3(s_ref, rows, cw_ref[...])
    o_ref[...] = (_dot(xb, wb_ref[...]) * y).astype(o_ref.dtype)
    s_ref[0:pad, :] = s_ref[rows:rows + pad, :]


def _mixer_in(x, w_in, conv_w, layer, *, seq):
    tokens, d = x.shape
    tm, tn = TILES["mixer_rows"], TILES["mixer_cols"]
    nj = d // tn
    kern = functools.partial(_mixer_in_kernel, tiles_per_seq=seq // tm)
    return pl.pallas_call(
        kern,
        grid=(nj, tokens // tm),
        in_specs=[
            pl.BlockSpec((tm, d), lambda j, i: (i, 0)),
            pl.BlockSpec((None, d, tn), lambda j, i: (layer, 0, j)),
            pl.BlockSpec((None, d, tn), lambda j, i: (layer, 0, j + nj)),
            pl.BlockSpec((None, d, tn), lambda j, i: (layer, 0, j + 2 * nj)),
            pl.BlockSpec((None, CONV_WIDTH, tn), lambda j, i: (layer, 0, j)),
        ],
        out_specs=pl.BlockSpec((tm, tn), lambda j, i: (i, j)),
        out_shape=jax.ShapeDtypeStruct((tokens, d), BF16),
        scratch_shapes=[pltpu.VMEM((tm + V7X_SUBLANES, tn), F32)],
        compiler_params=_compiler_params(("arbitrary", "arbitrary")),
        name="mixer_in",
    )(x, w_in, w_in, w_in, conv_w)


def _proj_ln_kernel(h_ref, w_ref, r_ref, g_ref, b_ref, *o_refs, alpha, row_chunks):
    rows = h_ref.shape[0] // row_chunks
    for c in range(row_chunks):
        sl = slice(c * rows, (c + 1) * rows)
        z = alpha * r_ref[sl, :] + _dot(h_ref[sl, :], w_ref[...])
        y = _layer_norm_rows(z, g_ref[...], b_ref[...])
        for o_ref in o_refs:
            o_ref[sl, :] = y.astype(o_ref.dtype)


def _proj_ln(h, w, w_idx, resid, ln_g, ln_b, ln_idx, *, alpha, tm, row_chunks, emit_bf16):
    tokens, d_in = h.shape
    d = w.shape[2]
    kern = functools.partial(_proj_ln_kernel, alpha=alpha, row_chunks=row_chunks)
    out_dtypes = (F32, BF16) if emit_bf16 else (F32,)
    return pl.pallas_call(
        kern,
        grid=(tokens // tm,),
        in_specs=[
            pl.BlockSpec((tm, d_in), lambda i: (i, 0)),
            pl.BlockSpec((None, d_in, d), lambda i: (w_idx, 0, 0), pipeline_mode=pl.Buffered(1)),
            pl.BlockSpec((tm, d), lambda i: (i, 0)),
            pl.BlockSpec((None, 1, d), lambda i: (ln_idx, 0, 0)),
            pl.BlockSpec((None, 1, d), lambda i: (ln_idx, 0, 0)),
        ],
        out_specs=tuple(pl.BlockSpec((tm, d), lambda i: (i, 0)) for _ in out_dtypes),
        out_shape=tuple(jax.ShapeDtypeStruct((tokens, d), dt) for dt in out_dtypes),
        compiler_params=_compiler_params(("arbitrary",)),
        name="proj_ln",
    )(h, w, resid, ln_g, ln_b)


def _ffn_up_kernel(x_ref, wg_ref, wu_ref, cwg_ref, cwu_ref, cbg_ref, cbu_ref, o_ref,
                   wgb_ref, wub_ref, sg_ref, su_ref, *, tiles_per_seq):
    i = pl.program_id(1)
    rows = x_ref.shape[0]
    pad = V7X_SUBLANES

    @pl.when(i == 0)
    def _():
        wgb_ref[...] = wg_ref[...].astype(BF16)
        wub_ref[...] = wu_ref[...].astype(BF16)

    @pl.when(i % tiles_per_seq == 0)
    def _():
        sg_ref[0:pad, :] = jnp.zeros((pad, sg_ref.shape[1]), F32)
        su_ref[0:pad, :] = jnp.zeros((pad, su_ref.shape[1]), F32)

    xb = x_ref[...].astype(BF16)
    sg_ref[pad:pad + rows, :] = _dot(xb, wgb_ref[...])
    su_ref[pad:pad + rows, :] = _dot(xb, wub_ref[...])
    gate = _causal_conv3(sg_ref, rows, cwg_ref[...]) + cbg_ref[...]
    up = _causal_conv3(su_ref, rows, cwu_ref[...]) + cbu_ref[...]
    o_ref[...] = (gate * jax.nn.sigmoid(gate) * up).astype(o_ref.dtype)
    sg_ref[0:pad, :] = sg_ref[rows:rows + pad, :]
    su_ref[0:pad, :] = su_ref[rows:rows + pad, :]


def _ffn_up(x, w_up, conv_w, conv_b, layer, *, seq):
    tokens, d = x.shape
    d_ff = w_up.shape[2] // 2
    tm, tn = TILES["ffn_up_rows"], TILES["ffn_up_cols"]
    nj = d_ff // tn
    kern = functools.partial(_ffn_up_kernel, tiles_per_seq=seq // tm)
    pad = V7X_SUBLANES
    return pl.pallas_call(
        kern,
        grid=(nj, tokens // tm),
        in_specs=[
            pl.BlockSpec((tm, d), lambda j, i: (i, 0)),
            pl.BlockSpec((None, d, tn), lambda j, i: (layer, 0, j)),
            pl.BlockSpec((None, d, tn), lambda j, i: (layer, 0, j + nj)),
            pl.BlockSpec((None, CONV_WIDTH, tn), lambda j, i: (layer, 0, j)),
            pl.BlockSpec((None, CONV_WIDTH, tn), lambda j, i: (layer, 0, j + nj)),
            pl.BlockSpec((None, 1, tn), lambda j, i: (layer, 0, j)),
            pl.BlockSpec((None, 1, tn), lambda j, i: (layer, 0, j + nj)),
        ],
        out_specs=pl.BlockSpec((tm, tn), lambda j, i: (i, j)),
        out_shape=jax.ShapeDtypeStruct((tokens, d_ff), BF16),
        scratch_shapes=[
            pltpu.VMEM((d, tn), BF16),
            pltpu.VMEM((d, tn), BF16),
            pltpu.VMEM((tm + pad, tn), F32),
            pltpu.VMEM((tm + pad, tn), F32),
        ],
        compiler_params=_compiler_params(("arbitrary", "arbitrary")),
        name="ffn_up",
    )(x, w_up, w_up, conv_w, conv_w, conv_b, conv_b)


def _qkv_rope_kernel(x_ref, w_ref, c_ref, s_ref, o_ref, *, q_scale, rot_dim, chunk):
    p = pl.program_id(0)
    rope = p < 2
    scale = jnp.where(p == 0, q_scale, 1.0).astype(F32)
    c_tab = jnp.where(rope, c_ref[...], 1.0) * scale
    s_tab = jnp.where(rope, s_ref[...], 0.0) * scale
    lane = lax.broadcasted_iota(jnp.int32, c_tab.shape, 1)
    first_half = lane < rot_dim // 2
    xb = x_ref[...].astype(BF16)
    for c0 in range(0, w_ref.shape[1], chunk):
        y = _dot(xb, w_ref[:, c0:c0 + chunk])
        for c in range(0, chunk, V7X_LANES):
            t = y[:, c:c + V7X_LANES]
            partner = jnp.where(first_half,
                                pltpu.roll(t, V7X_LANES - rot_dim // 2, 1),
                                pltpu.roll(t, rot_dim // 2, 1))
            o_ref[:, c0 + c:c0 + c + V7X_LANES] = (t * c_tab + partner * s_tab).astype(o_ref.dtype)


def _qkv_rope(x, w_stack, c_tab, s_tab, *, q_scale, rot_dim):
    tokens, d = x.shape
    n = w_stack.shape[0]
    tm = TILES["qkv_rows"]
    kern = functools.partial(_qkv_rope_kernel, q_scale=q_scale, rot_dim=rot_dim, chunk=TILES["qkv_col_chunk"])
    return pl.pallas_call(
        kern,
        grid=(n, tokens // tm),
        in_specs=[
            pl.BlockSpec((tm, d), lambda p, i: (i, 0)),
            pl.BlockSpec((None, d, d), lambda p, i: (p, 0, 0)),
            pl.BlockSpec((tm, V7X_LANES), lambda p, i: (i, 0)),
            pl.BlockSpec((tm, V7X_LANES), lambda p, i: (i, 0)),
        ],
        out_specs=pl.BlockSpec((None, tm, d), lambda p, i: (p, i, 0)),
        out_shape=jax.ShapeDtypeStruct((n, tokens, d), BF16),
        compiler_params=_compiler_params(("parallel", "parallel")),
        name="qkv_rope",
    )(x, w_stack, c_tab, s_tab)


def _diff_attn_kernel(q_ref, k_ref, v_ref, lam_ref, g_ref, o_ref, m_ref, l_ref, acc_ref,
                      *, lambda_init, tk):
    qi = pl.program_id(2)
    dh = HEAD_WIDTH // 2
    chains = q_ref.shape[1] // dh
    neg = jnp.finfo(F32).min

    m_ref[...] = jnp.full(m_ref.shape, neg, F32)
    l_ref[...] = jnp.zeros(l_ref.shape, F32)
    acc_ref[...] = jnp.zeros(acc_ref.shape, F32)

    def block(start, width, diag_col):
        for n in range(chains):
            hv = (n // 2) * HEAD_WIDTH
            s = lax.dot_general(q_ref[:, n * dh:(n + 1) * dh], k_ref[pl.ds(start, width), n * dh:(n + 1) * dh],
                                (((1,), (1,)), ((), ())), preferred_element_type=F32)
            if diag_col is not None:
                row = lax.broadcasted_iota(jnp.int32, s.shape, 0)
                col = lax.broadcasted_iota(jnp.int32, s.shape, 1)
                s = jnp.where(col <= row + diag_col, s, neg)
            m_prev = m_ref[n]
            m_next = jnp.maximum(m_prev, jnp.max(s, axis=1, keepdims=True))
            alpha = jnp.exp2(m_prev - m_next)
            p = jnp.exp2(s - jnp.concatenate([m_next] * (width // V7X_LANES), axis=1))
            l_ref[n] = alpha * l_ref[n] + jnp.sum(p, axis=1, keepdims=True)
            m_ref[n] = m_next
            acc_ref[n] = (acc_ref[n] * jnp.concatenate([alpha] * (HEAD_WIDTH // V7X_LANES), axis=1)
                          + _dot(p.astype(BF16), v_ref[pl.ds(start, width), hv:hv + HEAD_WIDTH]))

    def pair_body(j2, carry):
        block(pl.multiple_of(2 * j2 * tk, 2 * tk), tk, None)
        block(pl.multiple_of(2 * j2 * tk, 2 * tk) + tk, tk, None)
        return carry

    lax.fori_loop(0, qi // 2, pair_body, 0)

    @pl.when(qi % 2 == 1)
    def _():
        block(pl.multiple_of((qi - 1) * tk, tk), tk, None)
        block(pl.multiple_of(qi * tk, tk), tk, 0)

    @pl.when(qi % 2 == 0)
    def _():
        block(pl.multiple_of(qi * tk, tk), tk, 0)

    lam = lam_ref[...]
    lam_full = (jnp.exp(jnp.sum(lam[0:1] * lam[1:2], axis=1, keepdims=True))
                - jnp.exp(jnp.sum(lam[2:3] * lam[3:4], axis=1, keepdims=True)) + lambda_init)
    for h in range(chains // 2):
        inv = [jnp.concatenate([1.0 / l_ref[2 * h + c]] * (HEAD_WIDTH // V7X_LANES), axis=1) for c in range(2)]
        o = acc_ref[2 * h] * inv[0] - lam_full * (acc_ref[2 * h + 1] * inv[1])
        ms = jnp.mean(o * o, axis=-1, keepdims=True)
        o_ref[:, h * HEAD_WIDTH:(h + 1) * HEAD_WIDTH] = (
            o * lax.rsqrt(ms + SUBLN_EPS) * g_ref[...] * (1.0 - lambda_init)).astype(o_ref.dtype)


def _diff_attn(q_src, kv_src, lam, subln_g, idx, *, batch, seq, lambda_init):
    tokens, d = q_src.shape[1:]
    tq = TILES["attn_rows"]
    width = TILES["attn_heads"] * HEAD_WIDTH
    nq = seq // tq
    chains = 2 * TILES["attn_heads"]
    kern = functools.partial(_diff_attn_kernel, lambda_init=lambda_init, tk=tq)
    return pl.pallas_call(
        kern,
        grid=(batch, d // width, nq),
        in_specs=[
            pl.BlockSpec((None, tq, width), lambda b, h, i: (0, b * nq + i, h)),
            pl.BlockSpec((None, seq, width), lambda b, h, i: (1, b, h)),
            pl.BlockSpec((None, seq, width), lambda b, h, i: (2, b, h)),
            pl.BlockSpec((None,) + lam.shape[1:], lambda b, h, i: (idx, 0, 0)),
            pl.BlockSpec((None, 1, HEAD_WIDTH), lambda b, h, i: (idx, 0, 0)),
        ],
        out_specs=pl.BlockSpec((tq, width), lambda b, h, i: (b * nq + i, h)),
        out_shape=jax.ShapeDtypeStruct((tokens, d), BF16),
        scratch_shapes=[
            pltpu.VMEM((chains, tq, V7X_LANES), F32),
            pltpu.VMEM((chains, tq, V7X_LANES), F32),
            pltpu.VMEM((chains, tq, HEAD_WIDTH), F32),
        ],
        compiler_params=_compiler_params(("parallel", "parallel", "arbitrary")),
        name="diff_attn",
    )(q_src, kv_src, kv_src, lam, subln_g)


def kernel(x, positions, ln_g, ln_b, a_w_in, a_conv_w, a_w_out, kv_w_k, kv_w_v, b_w_q, b_lambda,
           b_subln_g, b_w_o, ffn_w_up, ffn_conv_w, ffn_conv_b, ffn_w_down):
    batch, seq, d = x.shape
    depth = ln_g.shape[0]
    n_a = a_w_in.shape[0]
    head_dim = HEAD_WIDTH // 2
    rot_dim = head_dim // 4
    alpha = (2.0 * depth) ** 0.25
    tokens = batch * seq

    a_w_in, a_w_out, b_w_o = a_w_in.astype(BF16), a_w_out.astype(BF16), b_w_o.astype(BF16)
    ffn_w_down = ffn_w_down.astype(BF16)
    ln_g = ln_g.reshape(2 * depth, 1, d)
    ln_b = ln_b.reshape(2 * depth, 1, d)
    ffn_conv_b = ffn_conv_b.reshape(depth, 1, -1)
    b_subln_g = b_subln_g.reshape(-1, 1, HEAD_WIDTH)

    xs = x.reshape(tokens, d)
    xs_mxu = xs
    c_tab, s_tab = _rope_tables(positions, rot_dim)
    kv_src = None
    for layer in range(depth):
        if layer < n_a:
            mix = _mixer_in(xs_mxu, a_w_in, a_conv_w, layer, seq=seq)
            w_proj, w_idx = a_w_out, layer
        else:
            j = layer - n_a
            if j == 0:
                w_stack = jnp.stack([b_w_q[j], kv_w_k, kv_w_v]).astype(BF16)
            else:
                w_stack = b_w_q[j][None].astype(BF16)
            q_src = _qkv_rope(xs_mxu, w_stack, c_tab, s_tab, q_scale=LOG2_E * head_dim ** -0.5, rot_dim=rot_dim)
            if j == 0:
                kv_src = q_src
            lambda_init = 0.8 - 0.6 * math.exp(-0.3 * layer)
            mix = _diff_attn(q_src, kv_src, b_lambda, b_subln_g, j, batch=batch, seq=seq,
                             lambda_init=lambda_init)
            w_proj, w_idx = b_w_o, j
        xs, xs_mxu = _proj_ln(mix, w_proj, w_idx, xs, ln_g, ln_b, 2 * layer, alpha=alpha,
                              tm=TILES["proj_rows"], row_chunks=TILES["proj_row_chunks"], emit_bf16=True)
        act = _ffn_up(xs_mxu, ffn_w_up, ffn_conv_w, ffn_conv_b, layer, seq=seq)
        outs = _proj_ln(act, ffn_w_down, layer, xs, ln_g, ln_b, 2 * layer + 1, alpha=alpha,
                        tm=TILES["ffn_down_rows"], row_chunks=TILES["ffn_down_row_chunks"],
                        emit_bf16=layer + 1 < depth)
        xs, xs_mxu = outs[0], outs[-1]
    return xs.reshape(batch, seq, d)
```
